```python
import jax, jax.numpy as jnp
from jax import lax
import numpy as np

D_MODEL = 2048
BATCH = 4
SEQ = 2048
DEPTH = 1

SWA_HEAD_DIM = 128
SWA_HEADS = (D_MODEL // 2) // SWA_HEAD_DIM
SWA_WIDTH = SWA_HEADS * SWA_HEAD_DIM
DILATED_PATTERNS = ((128, 1), (512, 4), (2048, 16))
SWA_BLOCK = 128

GLA_HEADS = 4
GLA_VALUE_DIM = (D_MODEL // 2) // GLA_HEADS
GLA_KEY_DIM = GLA_VALUE_DIM // 2
GLA_KEY_WIDTH = GLA_HEADS * GLA_KEY_DIM
GLA_VALUE_WIDTH = GLA_HEADS * GLA_VALUE_DIM
GLA_GATE_RANK = 16
GLA_GATE_TEMP = 16.0
GLA_CHUNK = 64

MIX_WIDTH = SWA_WIDTH + GLA_VALUE_WIDTH
IN_SPLITS = (
    SWA_WIDTH,
    2 * SWA_WIDTH,
    3 * SWA_WIDTH,
    3 * SWA_WIDTH + GLA_KEY_WIDTH,
    3 * SWA_WIDTH + 2 * GLA_KEY_WIDTH,
    3 * SWA_WIDTH + 2 * GLA_KEY_WIDTH + GLA_VALUE_WIDTH,
    3 * SWA_WIDTH + 2 * GLA_KEY_WIDTH + 2 * GLA_VALUE_WIDTH,
)
IN_WIDTH = IN_SPLITS[-1] + GLA_GATE_RANK

N_EXPERTS = 256
TOP_K = 8
N_GROUPS = 8
TOPK_GROUPS = 4
EXPERT_FF = 512
SHARED_FF = 512
ROUTED_SCALE = 2.5
DISPATCH_BLOCK = 64

DEEPNORM_ALPHA = (2.0 * DEPTH) ** 0.25
DEEPNORM_BETA = (8.0 * DEPTH) ** -0.25
LN_EPS = 1e-5
RMS_EPS = 1e-6
N_MOD = 6

kernel_name = "hybrid_dilated_swa_gla_moe_deepnorm_adaln"


def _layer_norm(x, g, b):
    xf = x.astype(jnp.float32)
    mu = xf.mean(-1, keepdims=True)
    var = jnp.square(xf - mu).mean(-1, keepdims=True)
    return ((xf - mu) * lax.rsqrt(var + LN_EPS)).astype(x.dtype) * g + b


def _dilated_window_attention(q, k, v, dilation, n_steps):
    b, s, h, dh = q.shape
    L = s // dilation
    nb = -(-L // SWA_BLOCK)
    lp = nb * SWA_BLOCK

    def to_residue(t):
        t = t.reshape(b, L, dilation, h, dh).transpose(0, 2, 3, 1, 4)
        t = jnp.pad(t, ((0, 0), (0, 0), (0, 0), (0, lp - L), (0, 0)))
        return t.reshape(b, dilation, h, nb, SWA_BLOCK, dh)

    def with_prev(t):
        prev = jnp.pad(t[:, :, :, :-1], ((0, 0), (0, 0), (0, 0), (1, 0), (0, 0), (0, 0)))
        return jnp.concatenate([prev, t], axis=4)

    qr, kr, vr = to_residue(q), to_residue(k), to_residue(v)
    kc, vc = with_prev(kr), with_prev(vr)
    scores = jnp.einsum('brhnqd,brhnkd->brhnqk', qr, kc).astype(jnp.float32) * (dh ** -0.5)
    qi = jnp.arange(SWA_BLOCK)[:, None]
    kj = jnp.arange(2 * SWA_BLOCK)[None, :]
    dist = qi + SWA_BLOCK - kj
    key_pos = jnp.arange(nb)[:, None, None] * SWA_BLOCK - SWA_BLOCK + kj[None]
    mask = (dist >= 0)[None] & (dist <= n_steps)[None] & (key_pos >= 0)
    scores = jnp.where(mask, scores, -jnp.inf)
    m = scores.max(-1, keepdims=True)
    p = jnp.exp(scores - m)
    l = p.sum(-1, keepdims=True)
    o = jnp.einsum('brhnqk,brhnkd->brhnqd', (p / l).astype(v.dtype), vc)
    lse = (m + jnp.log(l))[..., 0]
    o = o.reshape(b, dilation, h, lp, dh)[:, :, :, :L].transpose(0, 3, 1, 2, 4).reshape(b, s, h, dh)
    lse = lse.reshape(b, dilation, h, lp)[..., :L].transpose(0, 3, 1, 2).reshape(b, s, h)
    return o, lse


def _gla_chunked(q, k, v, log_a):
    b, s, h, dk = q.shape
    dv = v.shape[-1]
    n = s // GLA_CHUNK

    def chunks(t):
        return t.reshape(b, n, GLA_CHUNK, h, t.shape[-1]).transpose(1, 0, 3, 2, 4).astype(jnp.float32)

    qc, kc, vc, gc = chunks(q * (dk ** -0.5)), chunks(k), chunks(v), chunks(log_a)
    causal = jnp.tril(jnp.ones((GLA_CHUNK, GLA_CHUNK), bool))[:, :, None]

    def step(state, inp):
        qb, kb, vb, gb = inp
        cum = jnp.cumsum(gb, axis=2)
        total = cum[:, :, -1:]
        o_inter = jnp.einsum('bhck,bhkv->bhcv', qb * jnp.exp(cum), state)
        diff = cum[:, :, :, None, :] - cum[:, :, None, :, :]
        decay = jnp.exp(jnp.where(causal, diff, -jnp.inf))
        attn = jnp.einsum('bhik,bhjk,bhijk->bhij', qb, kb, decay)
        o_intra = jnp.einsum('bhij,bhjv->bhiv', attn, vb)
        new_state = jnp.exp(total)[:, :, 0, :, None] * state + jnp.einsum(
            'bhck,bhcv->bhkv', kb * jnp.exp(total - cum), vb)
        return new_state, o_inter + o_intra

    state0 = jnp.zeros((b, h, dk, dv), jnp.float32)
    _, o = lax.scan(step, state0, (qc, kc, vc, gc))
    return o.transpose(1, 0, 3, 2, 4).reshape(b, s, h, dv)


def _mixer(h, w_in, w_gla_gate, b_gla_gate, gla_norm_w, w_out):
    b, s, _ = h.shape
    proj = h @ w_in
    qa, ka, va, qg, kg, vg, rg, zg = jnp.split(proj, IN_SPLITS, axis=-1)

    qa = qa.reshape(b, s, SWA_HEADS, SWA_HEAD_DIM)
    ka = ka.reshape(b, s, SWA_HEADS, SWA_HEAD_DIM)
    va = va.reshape(b, s, SWA_HEADS, SWA_HEAD_DIM)
    outs, lses = [], []
    for window, dilation in DILATED_PATTERNS:
        o, lse = _dilated_window_attention(qa, ka, va, dilation, window // dilation)
        outs.append(o)
        lses.append(lse)
    mix_w = jax.nn.softmax(jnp.stack(lses, 0), axis=0)
    o_swa = jnp.einsum('pbsh,pbshd->bshd', mix_w, jnp.stack(outs, 0).astype(jnp.float32))
    o_swa = o_swa.reshape(b, s, SWA_WIDTH).astype(h.dtype)

    log_a = jax.nn.log_sigmoid((zg @ w_gla_gate + b_gla_gate).astype(jnp.float32)) / GLA_GATE_TEMP
    o_gla = _gla_chunked(qg.reshape(b, s, GLA_HEADS, GLA_KEY_DIM),
                         kg.reshape(b, s, GLA_HEADS, GLA_KEY_DIM),
                         vg.reshape(b, s, GLA_HEADS, GLA_VALUE_DIM),
                         log_a.reshape(b, s, GLA_HEADS, GLA_KEY_DIM))
    o_gla = o_gla * lax.rsqrt(jnp.square(o_gla).mean(-1, keepdims=True) + RMS_EPS)
    o_gla = o_gla.astype(h.dtype) * gla_norm_w
    o_gla = o_gla.reshape(b, s, GLA_VALUE_WIDTH) * jax.nn.silu(rg)

    return jnp.concatenate([o_swa, o_gla], axis=-1) @ w_out


def _route(h, w_router, router_bias):
    t = h.shape[0]
    scores = jax.nn.sigmoid(h.astype(jnp.float32) @ w_router.astype(jnp.float32))
    sel = scores + router_bias.astype(jnp.float32)
    grp = sel.reshape(t, N_GROUPS, N_EXPERTS // N_GROUPS)
    grp_score = lax.top_k(grp, 2)[0].sum(-1)
    _, top_groups = lax.top_k(grp_score, TOPK_GROUPS)
    group_mask = jax.nn.one_hot(top_groups, N_GROUPS, dtype=jnp.float32).sum(1) > 0
    expert_mask = jnp.repeat(group_mask, N_EXPERTS // N_GROUPS, axis=1)
    _, idx = lax.top_k(jnp.where(expert_mask, sel, -jnp.inf), TOP_K)
    w = jnp.take_along_axis(scores, idx, axis=1)
    w = w / w.sum(-1, keepdims=True) * ROUTED_SCALE
    return idx, w.astype(h.dtype)


def _routed_experts(h, idx, w, w_gate, w_up, w_down):
    t, d = h.shape
    a = t * TOP_K
    e_flat = idx.reshape(a)
    tok_flat = (jnp.arange(a, dtype=jnp.int32) // TOP_K)
    w_flat = w.reshape(a)
    order = jnp.argsort(e_flat)
    e_s, tok_s, w_s = e_flat[order], tok_flat[order], w_flat[order]
    counts = jnp.bincount(e_flat, length=N_EXPERTS)
    padded = (counts + DISPATCH_BLOCK - 1) // DISPATCH_BLOCK * DISPATCH_BLOCK
    starts = jnp.cumsum(counts) - counts
    pends = jnp.cumsum(padded)
    pstarts = pends - padded
    dest = pstarts[e_s] + (jnp.arange(a) - starts[e_s])
    n_rows = a + N_EXPERTS * DISPATCH_BLOCK
    n_blocks = n_rows // DISPATCH_BLOCK
    row_tok = jnp.full((n_rows,), t, jnp.int32).at[dest].set(tok_s)
    row_w = jnp.zeros((n_rows,), h.dtype).at[dest].set(w_s)
    block_e = jnp.minimum(
        jnp.searchsorted(pends, jnp.arange(n_blocks) * DISPATCH_BLOCK, side='right'), N_EXPERTS - 1)
    h_pad = jnp.concatenate([h, jnp.zeros((1, d), h.dtype)], axis=0)

    def block_fn(args):
        toks, wts, e = args
        xb = h_pad[toks]
        act = jax.nn.silu(xb @ w_gate[e]) * (xb @ w_up[e])
        return (act @ w_down[e]) * wts[:, None]

    y = lax.map(block_fn, (row_tok.reshape(n_blocks, DISPATCH_BLOCK),
                           row_w.reshape(n_blocks, DISPATCH_BLOCK), block_e))
    return jax.ops.segment_sum(y.reshape(n_rows, d), row_tok, num_segments=t + 1)[:t]


def _moe(h, w_router, router_bias, w_sh_gate, w_sh_up, w_sh_down, w_exp_gate, w_exp_up, w_exp_down):
    idx, w = _route(h, w_router, router_bias)
    shared = (jax.nn.silu(h @ w_sh_gate) * (h @ w_sh_up)) @ w_sh_down
    return shared + _routed_experts(h, idx, w, w_exp_gate, w_exp_up, w_exp_down)


def setup_inputs(seed: int = 0) -> dict:
    key = jax.random.key(seed)
    ks = jax.random.split(key, 24)
    f32 = jnp.float32
    D = D_MODEL

    def nrm(k, shape, scale):
        return jax.random.normal(k, shape, f32) * scale

    col_scale = jnp.concatenate([
        jnp.ones((2 * SWA_WIDTH,), f32), jnp.full((SWA_WIDTH,), DEEPNORM_BETA, f32),
        jnp.ones((2 * GLA_KEY_WIDTH,), f32), jnp.full((GLA_VALUE_WIDTH,), DEEPNORM_BETA, f32),
        jnp.ones((GLA_VALUE_WIDTH + GLA_GATE_RANK,), f32)])
    return {
        "x": nrm(ks[0], (BATCH, SEQ, D), 1.0),
        "c": nrm(ks[1], (BATCH, D), 1.0),
        "ln_in_g": 1.0 + nrm(ks[2], (D,), 0.02),
        "ln_in_b": nrm(ks[3], (D,), 0.02),
        "w_ada": nrm(ks[4], (DEPTH, D, N_MOD * D), 0.5 * D ** -0.5),
        "b_ada": nrm(ks[5], (DEPTH, N_MOD * D), 0.02),
        "w_in": nrm(ks[6], (DEPTH, D, IN_WIDTH), D ** -0.5) * col_scale,
        "w_gla_gate": nrm(ks[7], (DEPTH, GLA_GATE_RANK, GLA_KEY_WIDTH), GLA_GATE_RANK ** -0.5),
        "b_gla_gate": nrm(ks[8], (DEPTH, GLA_KEY_WIDTH), 0.1),
        "gla_norm_w": 1.0 + nrm(ks[9], (DEPTH, GLA_VALUE_DIM), 0.02),
        "w_out": nrm(ks[10], (DEPTH, MIX_WIDTH, D), DEEPNORM_BETA * MIX_WIDTH ** -0.5),
        "ln1_g": 1.0 + nrm(ks[11], (DEPTH, D), 0.02),
        "ln1_b": nrm(ks[12], (DEPTH, D), 0.02),
        "w_router": nrm(ks[13], (DEPTH, D, N_EXPERTS), D ** -0.5),
        "router_bias": nrm(ks[14], (DEPTH, N_EXPERTS), 0.01),
        "w_sh_gate": nrm(ks[15], (DEPTH, D, SHARED_FF), D ** -0.5),
        "w_sh_up": nrm(ks[16], (DEPTH, D, SHARED_FF), D ** -0.5),
        "w_sh_down": nrm(ks[17], (DEPTH, SHARED_FF, D), DEEPNORM_BETA * SHARED_FF ** -0.5),
        "w_exp_gate": nrm(ks[18], (DEPTH, N_EXPERTS, D, EXPERT_FF), D ** -0.5),
        "w_exp_up": nrm(ks[19], (DEPTH, N_EXPERTS, D, EXPERT_FF), D ** -0.5),
        "w_exp_down": nrm(ks[20], (DEPTH, N_EXPERTS, EXPERT_FF, D), DEEPNORM_BETA * EXPERT_FF ** -0.5),
        "ln2_g": 1.0 + nrm(ks[21], (DEPTH, D), 0.02),
        "ln2_b": nrm(ks[22], (DEPTH, D), 0.02),
    }


def reference(x, c, ln_in_g, ln_in_b, w_ada, b_ada, w_in, w_gla_gate, b_gla_gate, gla_norm_w, w_out,
              ln1_g, ln1_b, w_router, router_bias, w_sh_gate, w_sh_up, w_sh_down,
              w_exp_gate, w_exp_up, w_exp_down, ln2_g, ln2_b):
    b, s, d = x.shape
    x = _layer_norm(x, ln_in_g, ln_in_b)
    cond = jax.nn.silu(c)
    for l in range(DEPTH):
        mod = cond @ w_ada[l] + b_ada[l]
        sh1, sc1, g1, sh2, sc2, g2 = [m[:, None, :] for m in jnp.split(mod, N_MOD, axis=-1)]
        h = x * (1.0 + sc1) + sh1
        mix = _mixer(h, w_in[l], w_gla_gate[l], b_gla_gate[l], gla_norm_w[l], w_out[l])
        x = _layer_norm(DEEPNORM_ALPHA * x + g1 * mix, ln1_g[l], ln1_b[l])
        h = (x * (1.0 + sc2) + sh2).reshape(b * s, d)
        ffn = _moe(h, w_router[l], router_bias[l], w_sh_gate[l], w_sh_up[l], w_sh_down[l],
                   w_exp_gate[l], w_exp_up[l], w_exp_down[l]).reshape(b, s, d)
        x = _layer_norm(DEEPNORM_ALPHA * x + g2 * ffn, ln2_g[l], ln2_b[l])
    return x
```

```python
import functools

import numpy as np
import jax
import jax.numpy as jnp
from jax import lax
from jax.experimental import pallas as pl
from jax.experimental.pallas import tpu as pltpu

F32 = jnp.float32
BF16 = jnp.bfloat16
HIGHEST = lax.Precision.HIGHEST

SWA_HEAD_DIM = 128
SWA_HEADS = 8
SWA_WIDTH = SWA_HEADS * SWA_HEAD_DIM
DILATED_PATTERNS = ((128, 1), (512, 4), (2048, 16))
GLA_HEADS = 4
GLA_KEY_DIM = 128
GLA_VALUE_DIM = 256
GLA_KEY_WIDTH = GLA_HEADS * GLA_KEY_DIM
GLA_VALUE_WIDTH = GLA_HEADS * GLA_VALUE_DIM
GLA_GATE_RANK = 16
GLA_GATE_TEMP = 16.0
GLA_CHUNK = 64
N_EXPERTS = 256
TOP_K = 8
N_GROUPS = 8
GROUP_SIZE = N_EXPERTS // N_GROUPS
TOPK_GROUPS = 4
ROUTED_SCALE = 2.5
DEPTH = 1
DEEPNORM_ALPHA = (2.0 * DEPTH) ** 0.25
LN_EPS = 1e-5
RMS_EPS = 1e-6
N_MOD = 6
MAIN_WIDTH = 3 * SWA_WIDTH + 2 * GLA_KEY_WIDTH + 2 * GLA_VALUE_WIDTH

LANES = 128
VMEM_LIMIT = 56 * 1024 * 1024
ADA_TN = 1024
INPROJ_TM = 512
INPROJ_TN = 1024
ATTN_T = 256
GLA_ROWS = 512
OUTPROJ_TM = 256
ROUTER_TM = 256
MOE_RB = 128
COMBINE_TM = 128

NEG = -1e30
NT_DIMS = (((1,), (1,)), ((), ()))
TN_DIMS = (((0,), (0,)), ((), ()))


def _params(sem):
    return pltpu.CompilerParams(dimension_semantics=sem, vmem_limit_bytes=VMEM_LIMIT)


def _silu(x):
    return x * jax.nn.sigmoid(x)


def _layer_norm(x, g, b):
    mu = jnp.mean(x, axis=-1, keepdims=True)
    xc = x - mu
    var = jnp.mean(xc * xc, axis=-1, keepdims=True)
    return xc * lax.rsqrt(var + LN_EPS) * g + b


def _ada_kernel(c_ref, w_ref, b_ref, o_ref):
    a = _silu(c_ref[...])
    o_ref[...] = jnp.dot(a, w_ref[...], preferred_element_type=F32, precision=HIGHEST) + b_ref[...]


def _ada(c8, w, b):
    m, d = c8.shape
    n = w.shape[1]
    return pl.pallas_call(
        _ada_kernel,
        grid=(n // ADA_TN,),
        in_specs=[pl.BlockSpec((m, d), lambda j: (0, 0)),
                  pl.BlockSpec((d, ADA_TN), lambda j: (0, j)),
                  pl.BlockSpec((1, ADA_TN), lambda j: (0, j))],
        out_specs=pl.BlockSpec((m, ADA_TN), lambda j: (0, j)),
        out_shape=jax.ShapeDtypeStruct((m, n), F32),
        compiler_params=_params(("arbitrary",)),
        name="ada_mod",
    )(c8, w, b)


def _inproj_kernel(x_ref, g_ref, b_ref, sc_ref, sh_ref, w_ref, wz_ref, x0_ref, proj_ref, z_ref, h_scr):
    @pl.when(pl.program_id(2) == 0)
    def _():
        x0 = _layer_norm(x_ref[...], g_ref[...], b_ref[...])
        x0_ref[...] = x0
        hb = (x0 * (1.0 + sc_ref[...]) + sh_ref[...]).astype(BF16)
        h_scr[...] = hb
        z_ref[...] = jnp.dot(hb, wz_ref[...], preferred_element_type=F32)

    proj_ref[...] = jnp.dot(h_scr[...], w_ref[...], preferred_element_type=F32).astype(BF16)


def _inproj(x, ln_g, ln_b, sc, sh, w_main, w_z):
    bsz, s, d = x.shape
    n = w_main.shape[1]
    tm, tn = INPROJ_TM, INPROJ_TN
    return pl.pallas_call(
        _inproj_kernel,
        grid=(bsz, s // tm, n // tn),
        in_specs=[pl.BlockSpec((None, tm, d), lambda b, i, j: (b, i, 0)),
                  pl.BlockSpec((1, d), lambda b, i, j: (0, 0)),
                  pl.BlockSpec((1, d), lambda b, i, j: (0, 0)),
                  pl.BlockSpec((None, 1, d), lambda b, i, j: (b, 0, 0)),
                  pl.BlockSpec((None, 1, d), lambda b, i, j: (b, 0, 0)),
                  pl.BlockSpec((d, tn), lambda b, i, j: (0, j)),
                  pl.BlockSpec((d, LANES), lambda b, i, j: (0, 0))],
        out_specs=[pl.BlockSpec((None, tm, d), lambda b, i, j: (b, i, 0)),
                   pl.BlockSpec((None, tm, tn), lambda b, i, j: (b, i, j)),
                   pl.BlockSpec((None, tm, LANES), lambda b, i, j: (b, i, 0))],
        out_shape=[jax.ShapeDtypeStruct((bsz, s, d), F32),
                   jax.ShapeDtypeStruct((bsz, s, n), BF16),
                   jax.ShapeDtypeStruct((bsz, s, LANES), F32)],
        scratch_shapes=[pltpu.VMEM((tm, d), BF16)],
        compiler_params=_params(("arbitrary", "arbitrary", "arbitrary")),
        name="inproj",
    )(x, ln_g, ln_b, sc, sh, w_main, w_z)


def _attn_bias(s):
    t = ATTN_T
    qi = np.arange(t)[:, None]
    kj = np.arange(t)[None, :]
    out = np.empty((s // t, t, t), np.float32)
    for bd in range(s // t):
        delta = bd * t + qi - kj
        count = np.zeros((t, t), np.float64)
        for window, dilation in DILATED_PATTERNS:
            count += (delta >= 0) & (delta % dilation == 0) & (delta <= window)
        out[bd] = np.where(count > 0, np.log(np.maximum(count, 1.0)), NEG)
    return out


def _attn_kernel(q_ref, k_ref, v_ref, bias_ref, o_ref):
    t = ATTN_T
    qi = pl.program_id(2)
    q = q_ref[...]
    scale = SWA_HEAD_DIM ** -0.5

    def body(kb, carry):
        m, l, acc = carry
        off = pl.multiple_of(kb * t, t)
        k = k_ref[pl.ds(off, t), :]
        v = v_ref[pl.ds(off, t), :]
        s = lax.dot_general(q, k, NT_DIMS, preferred_element_type=F32) * scale + bias_ref[qi - kb]
        m_new = jnp.maximum(m, jnp.max(s, axis=-1, keepdims=True))
        alpha = jnp.exp(m - m_new)
        p = jnp.exp(s - m_new)
        l = alpha * l + jnp.sum(p, axis=-1, keepdims=True)
        acc = alpha * acc + jnp.dot(p.astype(BF16), v, preferred_element_type=F32)
        return m_new, l, acc

    init = (jnp.full((t, 1), NEG, F32), jnp.zeros((t, 1), F32), jnp.zeros((t, SWA_HEAD_DIM), F32))
    _, l, acc = lax.fori_loop(0, qi + 1, body, init)
    o_ref[...] = (acc / l).astype(BF16)


def _attention(proj, bias):
    bsz, s, _ = proj.shape
    t, hd = ATTN_T, SWA_HEAD_DIM
    nb = s // t
    return pl.pallas_call(
        _attn_kernel,
        grid=(bsz, SWA_HEADS, nb),
        in_specs=[pl.BlockSpec((None, t, hd), lambda b, h, i: (b, i, h)),
                  pl.BlockSpec((None, s, hd), lambda b, h, i: (b, 0, SWA_HEADS + h)),
                  pl.BlockSpec((None, s, hd), lambda b, h, i: (b, 0, 2 * SWA_HEADS + h)),
                  pl.BlockSpec((nb, t, t), lambda b, h, i: (0, 0, 0))],
        out_specs=pl.BlockSpec((None, t, hd), lambda b, h, i: (b, i, h)),
        out_shape=jax.ShapeDtypeStruct((bsz, s, SWA_WIDTH), BF16),
        compiler_params=_params(("arbitrary", "arbitrary", "arbitrary")),
        name="swa_attn",
    )(proj, proj, proj, bias)


def _gla_kernel(q_ref, k_ref, v_ref, r_ref, z_ref, wg_ref, bg_ref, nw_ref, o_ref, st_ref):
    c = GLA_CHUNK
    dk, dv = GLA_KEY_DIM, GLA_VALUE_DIM

    @pl.when(pl.program_id(1) == 0)
    def _():
        st_ref[...] = jnp.zeros_like(st_ref)

    row = lax.broadcasted_iota(jnp.int32, (c, c), 0)
    col = lax.broadcasted_iota(jnp.int32, (c, c), 1)
    causal = row >= col
    tril = causal.astype(F32)
    scale = dk ** -0.5

    def chunk(n, carry):
        off = pl.multiple_of(n * c, c)
        z = z_ref[pl.ds(off, c), :]
        for h in range(GLA_HEADS):
            ks = slice(h * dk, (h + 1) * dk)
            vs = slice(h * dv, (h + 1) * dv)
            zz = jnp.dot(z, wg_ref[:, ks], preferred_element_type=F32, precision=HIGHEST) + bg_ref[:, ks]
            log_a = (jnp.minimum(zz, 0.0) - jnp.log1p(jnp.exp(-jnp.abs(zz)))) * (1.0 / GLA_GATE_TEMP)
            cum = jnp.dot(tril, log_a, preferred_element_type=F32, precision=HIGHEST)
            total = cum[c - 1:c, :]
            q = q_ref[pl.ds(off, c), ks].astype(F32) * scale
            k = k_ref[pl.ds(off, c), ks].astype(F32)
            v = v_ref[pl.ds(off, c), vs]
            q_dec = (q * jnp.exp(cum)).astype(BF16)
            k_inv = (k * jnp.exp(-cum)).astype(BF16)
            k_rem = (k * jnp.exp(total - cum)).astype(BF16)
            state = st_ref[h]
            o_inter = lax.dot_general(q_dec, state.astype(BF16), NT_DIMS, preferred_element_type=F32)
            attn = lax.dot_general(q_dec, k_inv, NT_DIMS, preferred_element_type=F32)
            attn = jnp.where(causal, attn, 0.0).astype(BF16)
            o = o_inter + jnp.dot(attn, v, preferred_element_type=F32)
            st_ref[h] = state * jnp.exp(total) + lax.dot_general(v, k_rem, TN_DIMS, preferred_element_type=F32)
            o = o * lax.rsqrt(jnp.mean(o * o, axis=-1, keepdims=True) + RMS_EPS)
            o = o * nw_ref[...] * _silu(r_ref[pl.ds(off, c), vs].astype(F32))
            o_ref[pl.ds(off, c), vs] = o.astype(BF16)
        return carry

    lax.fori_loop(0, GLA_ROWS // c, chunk, 0)


def _gla(proj, z, wg_pad, bg, nw):
    bsz, s, _ = proj.shape
    rows = GLA_ROWS
    kw, vw = GLA_KEY_WIDTH, GLA_VALUE_WIDTH
    q_blk = 3 * SWA_WIDTH // kw
    v_blk = (3 * SWA_WIDTH + 2 * kw) // vw
    return pl.pallas_call(
        _gla_kernel,
        grid=(bsz, s // rows),
        in_specs=[pl.BlockSpec((None, rows, kw), lambda b, i: (b, i, q_blk)),
                  pl.BlockSpec((None, rows, kw), lambda b, i: (b, i, q_blk + 1)),
                  pl.BlockSpec((None, rows, vw), lambda b, i: (b, i, v_blk)),
                  pl.BlockSpec((None, rows, vw), lambda b, i: (b, i, v_blk + 1)),
                  pl.BlockSpec((None, rows, LANES), lambda b, i: (b, i, 0)),
                  pl.BlockSpec((LANES, kw), lambda b, i: (0, 0)),
                  pl.BlockSpec((1, kw), lambda b, i: (0, 0)),
                  pl.BlockSpec((1, GLA_VALUE_DIM), lambda b, i: (0, 0))],
        out_specs=pl.BlockSpec((None, rows, vw), lambda b, i: (b, i, 0)),
        out_shape=jax.ShapeDtypeStruct((bsz, s, vw), BF16),
        scratch_shapes=[pltpu.VMEM((GLA_HEADS, GLA_VALUE_DIM, GLA_KEY_DIM), F32)],
        compiler_params=_params(("arbitrary", "arbitrary")),
        name="gla",
    )(proj, proj, proj, proj, z, wg_pad, bg, nw)


def _outproj_kernel(oa_ref, ob_ref, wa_ref, wb_ref, x0_ref, g1_ref, sc_ref, sh_ref, lg_ref, lb_ref,
                    x1_ref, h2_ref):
    mix = (jnp.dot(oa_ref[...], wa_ref[...], preferred_element_type=F32)
           + jnp.dot(ob_ref[...], wb_ref[...], preferred_element_type=F32))
    x1 = _layer_norm(DEEPNORM_ALPHA * x0_ref[...] + g1_ref[...] * mix, lg_ref[...], lb_ref[...])
    x1_ref[...] = x1
    h2_ref[...] = x1 * (1.0 + sc_ref[...]) + sh_ref[...]


def _outproj(o_swa, o_gla, w_a, w_b, x0, g1, sc2, sh2, ln_g, ln_b):
    bsz, s, d = x0.shape
    tm = OUTPROJ_TM
    ka, kb = o_swa.shape[-1], o_gla.shape[-1]
    row = lambda b, i: (b, i, 0)
    per_batch = lambda b, i: (b, 0, 0)
    const2 = lambda b, i: (0, 0)
    return pl.pallas_call(
        _outproj_kernel,
        grid=(bsz, s // tm),
        in_specs=[pl.BlockSpec((None, tm, ka), row),
                  pl.BlockSpec((None, tm, kb), row),
                  pl.BlockSpec((ka, d), const2),
                  pl.BlockSpec((kb, d), const2),
                  pl.BlockSpec((None, tm, d), row),
                  pl.BlockSpec((None, 1, d), per_batch),
                  pl.BlockSpec((None, 1, d), per_batch),
                  pl.BlockSpec((None, 1, d), per_batch),
                  pl.BlockSpec((1, d), const2),
                  pl.BlockSpec((1, d), const2)],
        out_specs=[pl.BlockSpec((None, tm, d), row), pl.BlockSpec((None, tm, d), row)],
        out_shape=[jax.ShapeDtypeStruct((bsz, s, d), F32), jax.ShapeDtypeStruct((bsz, s, d), F32)],
        compiler_params=_params(("arbitrary", "arbitrary")),
        name="outproj_ln1",
    )(o_swa, o_gla, w_a, w_b, x0, g1, sc2, sh2, ln_g, ln_b)


def _router_kernel(h_ref, wr_ref, rb_ref, wsg_ref, wsu_ref, wsd_ref, idx_ref, wts_ref, sh_ref):
    h = h_ref[...]
    tm = h.shape[0]
    logits = lax.dot_general(wr_ref[...], h, NT_DIMS, preferred_element_type=F32, precision=HIGHEST)
    scores = jax.nn.sigmoid(logits)
    sel = scores + rb_ref[...]

    group_scores = []
    for g in range(N_GROUPS):
        sg = sel[g * GROUP_SIZE:(g + 1) * GROUP_SIZE, :]
        m1 = jnp.max(sg, axis=0, keepdims=True)
        dup = jnp.sum((sg == m1).astype(F32), axis=0, keepdims=True)
        m2 = jnp.max(jnp.where(sg < m1, sg, -jnp.inf), axis=0, keepdims=True)
        group_scores.append(m1 + jnp.where(dup >= 2.0, m1, m2))
    gs = jnp.concatenate(group_scores, axis=0)

    gidx = lax.broadcasted_iota(jnp.int32, gs.shape, 0)
    rank = jnp.zeros(gs.shape, F32)
    for g in range(N_GROUPS):
        other = gs[g:g + 1, :]
        ahead = (other > gs) | ((other == gs) & (g < gidx))
        rank = rank + ahead.astype(F32)
    keep = rank < float(TOPK_GROUPS)
    masked = jnp.concatenate(
        [jnp.where(keep[g:g + 1, :], sel[g * GROUP_SIZE:(g + 1) * GROUP_SIZE, :], -jnp.inf)
         for g in range(N_GROUPS)], axis=0)

    eidx = lax.broadcasted_iota(jnp.int32, masked.shape, 0).astype(F32)
    idx_rows, w_rows = [], []
    for _ in range(TOP_K):
        m = jnp.max(masked, axis=0, keepdims=True)
        first = jnp.min(jnp.where(masked == m, eidx, float(N_EXPERTS)), axis=0, keepdims=True)
        hit = eidx == first
        w_rows.append(jnp.sum(jnp.where(hit, scores, 0.0), axis=0, keepdims=True))
        idx_rows.append(first)
        masked = jnp.where(hit, -jnp.inf, masked)
    w = jnp.concatenate(w_rows, axis=0)
    w = w / jnp.sum(w, axis=0, keepdims=True) * ROUTED_SCALE
    idx_ref[...] = jnp.concatenate(idx_rows, axis=0).astype(jnp.int32)
    wts_ref[...] = w

    hb = h.astype(BF16)
    act = _silu(jnp.dot(hb, wsg_ref[...], preferred_element_type=F32)) * jnp.dot(
        hb, wsu_ref[...], preferred_element_type=F32)
    sh_ref[...] = jnp.dot(act.astype(BF16), wsd_ref[...], preferred_element_type=F32)


def _router(h2, wr_t, rbias, wsg, wsu, wsd):
    t, d = h2.shape
    tm = ROUTER_TM
    ff = wsg.shape[1]
    const = lambda i: (0, 0)
    return pl.pallas_call(
        _router_kernel,
        grid=(t // tm,),
        in_specs=[pl.BlockSpec((tm, d), lambda i: (i, 0)),
                  pl.BlockSpec((N_EXPERTS, d), const),
                  pl.BlockSpec((N_EXPERTS, 1), const),
                  pl.BlockSpec((d, ff), const),
                  pl.BlockSpec((d, ff), const),
                  pl.BlockSpec((ff, d), const)],
        out_specs=[pl.BlockSpec((TOP_K, tm), lambda i: (0, i)),
                   pl.BlockSpec((TOP_K, tm), lambda i: (0, i)),
                   pl.BlockSpec((tm, d), lambda i: (i, 0))],
        out_shape=[jax.ShapeDtypeStruct((TOP_K, t), jnp.int32),
                   jax.ShapeDtypeStruct((TOP_K, t), F32),
                   jax.ShapeDtypeStruct((t, d), F32)],
        compiler_params=_params(("arbitrary",)),
        name="router_shared",
    )(h2, wr_t, rbias, wsg, wsu, wsd)


def _moe_kernel(starts_ref, order_ref, h_hbm, wg_ref, wu_ref, wd_ref, y_hbm,
                wg_bf, wu_bf, wd_bf, xbuf, ybuf, gsem, ssem, *, n_tokens):
    rb = MOE_RB
    e = pl.program_id(0)
    start = starts_ref[e]
    n = starts_ref[e + 1] - start

    @pl.when(e == 0)
    def _():
        xbuf[...] = jnp.zeros_like(xbuf)

    def gather_copy(tok, r):
        return pltpu.make_async_copy(h_hbm.at[pl.ds(tok, 1), :], xbuf.at[pl.ds(r, 1), :], gsem)

    def scatter_copy(r, slot):
        return pltpu.make_async_copy(ybuf.at[pl.ds(r, 1), :], y_hbm.at[pl.ds(slot, 1), :], ssem)

    @pl.when(n > 0)
    def _():
        wg_bf[...] = wg_ref[...].astype(BF16)
        wu_bf[...] = wu_ref[...].astype(BF16)
        wd_bf[...] = wd_ref[...].astype(BF16)

        def block(b, carry):
            base = start + b * rb
            m = jnp.minimum(rb, n - b * rb)

            def issue_gather(r, c):
                gather_copy(lax.rem(order_ref[base + r], n_tokens), r).start()
                return c

            def wait_gather(r, c):
                gather_copy(0, 0).wait()
                return c

            lax.fori_loop(0, m, issue_gather, 0)
            lax.fori_loop(0, m, wait_gather, 0)
            x = xbuf[...].astype(BF16)
            gate = jnp.dot(x, wg_bf[...], preferred_element_type=F32)
            up = jnp.dot(x, wu_bf[...], preferred_element_type=F32)
            act = (_silu(gate) * up).astype(BF16)
            ybuf[...] = jnp.dot(act, wd_bf[...], preferred_element_type=F32)

            def issue_scatter(r, c):
                scatter_copy(r, order_ref[base + r]).start()
                return c

            def wait_scatter(r, c):
                scatter_copy(0, 0).wait()
                return c

            lax.fori_loop(0, m, issue_scatter, 0)
            lax.fori_loop(0, m, wait_scatter, 0)
            return carry

        lax.fori_loop(0, (n + rb - 1) // rb, block, 0)


def _moe(starts, order, h2, wg, wu, wd):
    t, d = h2.shape
    ne, _, ff = wg.shape
    rb = MOE_RB
    grid_spec = pltpu.PrefetchScalarGridSpec(
        num_scalar_prefetch=2,
        grid=(ne,),
        in_specs=[pl.BlockSpec(memory_space=pl.ANY),
                  pl.BlockSpec((None, d, ff), lambda e, s, o: (e, 0, 0)),
                  pl.BlockSpec((None, d, ff), lambda e, s, o: (e, 0, 0)),
                  pl.BlockSpec((None, ff, d), lambda e, s, o: (e, 0, 0))],
        out_specs=pl.BlockSpec(memory_space=pl.ANY),
        scratch_shapes=[pltpu.VMEM((d, ff), BF16), pltpu.VMEM((d, ff), BF16), pltpu.VMEM((ff, d), BF16),
                        pltpu.VMEM((rb, d), F32), pltpu.VMEM((rb, d), F32),
                        pltpu.SemaphoreType.DMA(()), pltpu.SemaphoreType.DMA(())])
    return pl.pallas_call(
        functools.partial(_moe_kernel, n_tokens=t),
        grid_spec=grid_spec,
        out_shape=jax.ShapeDtypeStruct((TOP_K * t, d), F32),
        compiler_params=_params(("arbitrary",)),
        name="moe_experts",
    )(starts, order, h2, wg, wu, wd)


def _combine_kernel(y_ref, w_ref, sh_ref, x1_ref, g2_ref, lg_ref, lb_ref, o_ref):
    w = w_ref[...]
    ffn = sh_ref[...]
    for k in range(TOP_K):
        ffn = ffn + w[:, k:k + 1] * y_ref[k]
    o_ref[...] = _layer_norm(DEEPNORM_ALPHA * x1_ref[...] + g2_ref[...] * ffn, lg_ref[...], lb_ref[...])


def _combine(y, w_t, shared, x1, g2, ln_g, ln_b):
    bsz, s, d = x1.shape
    tm = COMBINE_TM
    spb = s // tm
    return pl.pallas_call(
        _combine_kernel,
        grid=(bsz, spb),
        in_specs=[pl.BlockSpec((TOP_K, tm, d), lambda b, i: (0, b * spb + i, 0)),
                  pl.BlockSpec((tm, TOP_K), lambda b, i: (b * spb + i, 0)),
                  pl.BlockSpec((tm, d), lambda b, i: (b * spb + i, 0)),
                  pl.BlockSpec((None, tm, d), lambda b, i: (b, i, 0)),
                  pl.BlockSpec((None, 1, d), lambda b, i: (b, 0, 0)),
                  pl.BlockSpec((1, d), lambda b, i: (0, 0)),
                  pl.BlockSpec((1, d), lambda b, i: (0, 0))],
        out_specs=pl.BlockSpec((None, tm, d), lambda b, i: (b, i, 0)),
        out_shape=jax.ShapeDtypeStruct((bsz, s, d), F32),
        compiler_params=_params(("arbitrary", "arbitrary")),
        name="combine_ln2",
    )(y, w_t, shared, x1, g2, ln_g, ln_b)


def kernel(x, c, ln_in_g, ln_in_b, w_ada, b_ada, w_in, w_gla_gate, b_gla_gate, gla_norm_w, w_out, ln1_g, ln1_b, w_router, router_bias, w_sh_gate, w_sh_up, w_sh_down, w_exp_gate, w_exp_up, w_exp_down, ln2_g, ln2_b):
    bsz, s, d = x.shape
    t = bsz * s
    assert w_ada.shape[0] == DEPTH
    assert s % ATTN_T == 0 and s % GLA_ROWS == 0 and s % INPROJ_TM == 0
    row = lambda v: v.reshape(1, -1)

    c8 = jnp.pad(c, ((0, 8 - bsz), (0, 0)))
    mod = _ada(c8, w_ada[0], row(b_ada[0]))[:bsz]
    sh1, sc1, g1, sh2, sc2, g2 = [m.reshape(bsz, 1, d) for m in jnp.split(mod, N_MOD, axis=-1)]

    w_main = w_in[0][:, :MAIN_WIDTH].astype(BF16)
    w_z = jnp.pad(w_in[0][:, MAIN_WIDTH:], ((0, 0), (0, LANES - GLA_GATE_RANK))).astype(BF16)
    x0, proj, z = _inproj(x, row(ln_in_g), row(ln_in_b), sc1, sh1, w_main, w_z)

    o_swa = _attention(proj, jnp.asarray(_attn_bias(s)))
    wg_pad = jnp.pad(w_gla_gate[0], ((0, LANES - GLA_GATE_RANK), (0, 0)))
    o_gla = _gla(proj, z, wg_pad, row(b_gla_gate[0]), row(gla_norm_w[0]))

    w_o = w_out[0].astype(BF16)
    x1, h2 = _outproj(o_swa, o_gla, w_o[:SWA_WIDTH], w_o[SWA_WIDTH:], x0, g1, sc2, sh2,
                      row(ln1_g[0]), row(ln1_b[0]))
    h2 = h2.reshape(t, d)

    idx, wts, shared = _router(h2, w_router[0].T, router_bias[0].reshape(-1, 1),
                               w_sh_gate[0].astype(BF16), w_sh_up[0].astype(BF16), w_sh_down[0].astype(BF16))

    flat = idx.reshape(-1)
    order = jnp.argsort(flat).astype(jnp.int32)
    counts = jnp.bincount(flat, length=N_EXPERTS).astype(jnp.int32)
    starts = jnp.concatenate([jnp.zeros((1,), jnp.int32), jnp.cumsum(counts).astype(jnp.int32)])

    y = _moe(starts, order, h2, w_exp_gate[0], w_exp_up[0], w_exp_down[0])
    return _combine(y.reshape(TOP_K, t, d), wts.T, shared, x1, g2, row(ln2_g[0]), row(ln2_b[0]))
```
